```python
import jax, jax.numpy as jnp
from jax import lax
import numpy as np

D_MODEL = 4096
BATCH = 1
SEQ = 8192
DEPTH = 4

N_MIXERS = 4
HEAD_DIM = 128
N_HEADS = D_MODEL // HEAD_DIM
N_KV_HEADS = N_HEADS // 4
GQA_GROUP = N_HEADS // N_KV_HEADS
QKV_DIM = (N_HEADS + 2 * N_KV_HEADS) * HEAD_DIM
Q_BLOCK = 128
ROPE_THETA = 500000.0
PARTIAL_ROPE_DIM = HEAD_DIM // 4
AXIAL_THETA = 10000.0
GRID_W = 64
WINDOW_HALF = 128
MLA_HEADS = 32
MLA_Q_RANK = 1024
MLA_KV_RANK = 512
MLA_NOPE_DIM = 128
MLA_ROPE_DIM = 64
MLA_V_DIM = 128
DIL_PATTERNS = ((128, 1), (512, 4), (2048, 16))
DIL_KV_HEADS = (2, 3, 3)
DIL_N_GROUPS = 3
DIL_MAX_HALF = 1024
FFN_DIM = 14336
N_EXPERTS = 8
TOP_K = 2
EXPERT_DIM = 1792
RMS_EPS = 1e-6

kernel_name = 'hybrid_interleaved_encoder'


def rms_norm(x, g):
    xf = x.astype(jnp.float32)
    y = xf * lax.rsqrt(jnp.mean(xf * xf, axis=-1, keepdims=True) + RMS_EPS)
    return (y * g.astype(jnp.float32)).astype(x.dtype)


def rope_angles(pos, dim, theta):
    inv = 1.0 / (theta ** (jnp.arange(0, dim, 2, dtype=jnp.float32) / dim))
    return pos.astype(jnp.float32)[:, None] * inv[None, :]


def apply_rope(x, ang):
    r2 = x.shape[-1] // 2
    cos = jnp.cos(ang)[None, :, None, :].astype(x.dtype)
    sin = jnp.sin(ang)[None, :, None, :].astype(x.dtype)
    x1, x2 = x[..., :r2], x[..., r2:]
    return jnp.concatenate([x1 * cos - x2 * sin, x1 * sin + x2 * cos], axis=-1)


def partial_rope(x, ang):
    return jnp.concatenate([apply_rope(x[..., :PARTIAL_ROPE_DIM], ang), x[..., PARTIAL_ROPE_DIM:]], axis=-1)


def qkv_heads(h, wqkv):
    B, S, _ = h.shape
    qkv = h @ wqkv
    nq, nk = N_HEADS * HEAD_DIM, N_KV_HEADS * HEAD_DIM
    q = qkv[..., :nq].reshape(B, S, N_HEADS, HEAD_DIM)
    k = qkv[..., nq:nq + nk].reshape(B, S, N_KV_HEADS, HEAD_DIM)
    v = qkv[..., nq + nk:].reshape(B, S, N_KV_HEADS, HEAD_DIM)
    return q, k, v


def dense_block_attention(q, k, v, scale):
    B, S, KV, G, dk = q.shape
    nb = S // Q_BLOCK
    qb = jnp.moveaxis(q.reshape(B, nb, Q_BLOCK, KV, G, dk), 1, 0)

    def one_block(qi):
        s = jnp.einsum('bqkgd,bskd->bkgqs', qi, k, preferred_element_type=jnp.float32) * scale
        p = jax.nn.softmax(s, axis=-1).astype(v.dtype)
        return jnp.einsum('bkgqs,bskd->bqkgd', p, v)

    o = lax.map(one_block, qb)
    return jnp.moveaxis(o, 0, 1).reshape(B, S, KV * G, v.shape[-1])


def mixer_axial_gqa(h, wqkv, q_gain, k_gain, wo):
    B, S, _ = h.shape
    q, k, v = qkv_heads(h, wqkv)
    q, k = rms_norm(q, q_gain), rms_norm(k, k_gain)
    rows = S // GRID_W
    row = jnp.repeat(jnp.arange(rows, dtype=jnp.int32), GRID_W)
    col = jnp.tile(jnp.arange(GRID_W, dtype=jnp.int32), rows)
    half = HEAD_DIM // 2
    ang_r = rope_angles(row, half, AXIAL_THETA)
    ang_c = rope_angles(col, half, AXIAL_THETA)

    def axial(t):
        return jnp.concatenate([apply_rope(t[..., :half], ang_r), apply_rope(t[..., half:], ang_c)], axis=-1)

    q, k = axial(q), axial(k)
    o = dense_block_attention(q.reshape(B, S, N_KV_HEADS, GQA_GROUP, HEAD_DIM), k, v, HEAD_DIM ** -0.5)
    return o.reshape(B, S, N_HEADS * HEAD_DIM) @ wo


def mixer_window_sink(h, wqkv, sink, wo, ang):
    B, S, _ = h.shape
    q, k, v = qkv_heads(h, wqkv)
    q, k = partial_rope(q, ang), partial_rope(k, ang)
    nb = S // Q_BLOCK

    def band(t):
        tp = jnp.pad(t, ((0, 0), (Q_BLOCK, Q_BLOCK), (0, 0), (0, 0)))
        tp = tp.reshape(B, nb + 2, Q_BLOCK, N_KV_HEADS, HEAD_DIM)
        tb = jnp.concatenate([tp[:, :-2], tp[:, 1:-1], tp[:, 2:]], axis=2)
        return jnp.moveaxis(tb, 1, 0)

    kb, vb = band(k), band(v)
    qb = jnp.moveaxis(q.reshape(B, nb, Q_BLOCK, N_KV_HEADS, GQA_GROUP, HEAD_DIM), 1, 0)
    rel = jnp.arange(3 * Q_BLOCK)[None, :] - Q_BLOCK - jnp.arange(Q_BLOCK)[:, None]
    in_band = jnp.abs(rel) <= WINDOW_HALF
    sink_l = sink.astype(jnp.float32).reshape(N_KV_HEADS, GQA_GROUP)[None, :, :, None, None]
    scale = HEAD_DIM ** -0.5

    def one_block(args):
        qi, ki, vi, start = args
        s = jnp.einsum('bqkgd,bskd->bkgqs', qi, ki, preferred_element_type=jnp.float32) * scale
        kpos = start - Q_BLOCK + jnp.arange(3 * Q_BLOCK)
        valid = in_band & ((kpos >= 0) & (kpos < S))[None, :]
        s = jnp.where(valid, s, -jnp.inf)
        sk = jnp.broadcast_to(sink_l, s.shape[:-1] + (1,))
        p = jax.nn.softmax(jnp.concatenate([s, sk], axis=-1), axis=-1)[..., :-1]
        return jnp.einsum('bkgqs,bskd->bqkgd', p.astype(vi.dtype), vi)

    starts = jnp.arange(nb, dtype=jnp.int32) * Q_BLOCK
    o = lax.map(one_block, (qb, kb, vb, starts))
    return jnp.moveaxis(o, 0, 1).reshape(B, S, N_HEADS * HEAD_DIM) @ wo


def mixer_mla(h, wdq, q_gain, wuq, wdkv, kv_gain, wukv, wo, ang):
    B, S, _ = h.shape
    cq = rms_norm(h @ wdq, q_gain)
    q = (cq @ wuq).reshape(B, S, MLA_HEADS, MLA_NOPE_DIM + MLA_ROPE_DIM)
    q_nope, q_rope = q[..., :MLA_NOPE_DIM], apply_rope(q[..., MLA_NOPE_DIM:], ang)
    dkv = h @ wdkv
    ckv = rms_norm(dkv[..., :MLA_KV_RANK], kv_gain)
    k_rope = apply_rope(dkv[..., None, MLA_KV_RANK:], ang)
    kv = (ckv @ wukv).reshape(B, S, MLA_HEADS, MLA_NOPE_DIM + MLA_V_DIM)
    k_nope, v = kv[..., :MLA_NOPE_DIM], kv[..., MLA_NOPE_DIM:]
    qf = jnp.concatenate([q_nope, q_rope], axis=-1)[:, :, :, None, :]
    kf = jnp.concatenate([k_nope, jnp.broadcast_to(k_rope, (B, S, MLA_HEADS, MLA_ROPE_DIM))], axis=-1)
    o = dense_block_attention(qf, kf, v, (MLA_NOPE_DIM + MLA_ROPE_DIM) ** -0.5)
    return o.reshape(B, S, MLA_HEADS * MLA_V_DIM) @ wo


def dilated_offsets():
    rows = []
    for (w, d), nkv in zip(DIL_PATTERNS, DIL_KV_HEADS):
        m = np.arange(1, (w // 2) // d + 1)
        off = np.concatenate([-m[::-1] * d, np.zeros(1, dtype=m.dtype), m * d])
        rows += [off] * nkv
    return np.stack(rows).astype(np.int32)


def mixer_dilated(h, wqkv, wo, ang):
    B, S, _ = h.shape
    q, k, v = qkv_heads(h, wqkv)
    q, k = partial_rope(q, ang), partial_rope(k, ang)
    off = jnp.asarray(dilated_offsets())
    kv_idx = jnp.arange(N_KV_HEADS, dtype=jnp.int32)[:, None]
    pad = ((0, 0), (DIL_MAX_HALF, DIL_MAX_HALF), (0, 0), (0, 0))
    kp, vp = jnp.pad(k, pad), jnp.pad(v, pad)
    nb = S // Q_BLOCK
    qb = jnp.moveaxis(q.reshape(B, nb, Q_BLOCK, N_KV_HEADS, GQA_GROUP, HEAD_DIM), 1, 0)
    scale = HEAD_DIM ** -0.5

    def one_block(args):
        qi, start = args
        pos = start + jnp.arange(Q_BLOCK, dtype=jnp.int32)[:, None, None] + off[None]
        idx = pos + DIL_MAX_HALF
        kg = kp[:, idx, kv_idx]
        vg = vp[:, idx, kv_idx]
        s = jnp.einsum('bqkgd,bqknd->bqkgn', qi, kg, preferred_element_type=jnp.float32) * scale
        valid = ((pos >= 0) & (pos < S))[None, :, :, None, :]
        s = jnp.where(valid, s, -jnp.inf)
        m = jnp.max(s, axis=-1, keepdims=True)
        e = jnp.exp(s - m)
        den = jnp.sum(e, axis=-1, keepdims=True)
        o = jnp.einsum('bqkgn,bqknd->bqkgd', (e / den).astype(vg.dtype), vg)
        return o, (m + jnp.log(den))[..., 0]

    starts = jnp.arange(nb, dtype=jnp.int32) * Q_BLOCK
    o, lse = lax.map(one_block, (qb, starts))
    o = jnp.moveaxis(o, 0, 1).reshape(B, S, N_HEADS, HEAD_DIM)
    lse = jnp.moveaxis(lse, 0, 1).reshape(B, S, N_HEADS)
    heads_per_group = np.array([n * GQA_GROUP for n in DIL_KV_HEADS])
    head_group = np.repeat(np.arange(DIL_N_GROUPS), heads_per_group)
    member = jnp.asarray(head_group[None, :] == np.arange(DIL_N_GROUPS)[:, None])
    masked = jnp.where(member, lse[:, :, None, :], -jnp.inf)
    log_den = jax.nn.logsumexp(masked, axis=-1) - jnp.log(jnp.asarray(heads_per_group, jnp.float32))
    alpha = jax.nn.softmax(log_den, axis=-1)
    wgt = alpha[..., jnp.asarray(head_group)] * DIL_N_GROUPS
    o = o * wgt[..., None].astype(o.dtype)
    return o.reshape(B, S, N_HEADS * HEAD_DIM) @ wo


def swiglu(h, wg, wu, wd):
    return (jax.nn.silu(h @ wg) * (h @ wu)) @ wd


def moe_swiglu(h, router, wg, wu, wd):
    logits = (h @ router).astype(jnp.float32)
    top_v, top_i = lax.top_k(logits, TOP_K)
    gates = jax.nn.softmax(top_v, axis=-1)
    out = jnp.zeros_like(h)
    for e in range(N_EXPERTS):
        g_e = jnp.sum(jnp.where(top_i == e, gates, 0.0), axis=-1)
        out = out + g_e[..., None].astype(h.dtype) * swiglu(h, wg[e], wu[e], wd[e])
    return out


def setup_inputs(seed: int = 0) -> dict:
    key = jax.random.key(seed)
    ks = list(jax.random.split(key, 32))

    def w(shape, fan_in):
        return jax.random.normal(ks.pop(), shape, jnp.float32) * fan_in ** -0.5

    def gain(shape):
        return 1.0 + 0.01 * jax.random.normal(ks.pop(), shape, jnp.float32)

    n_dense, n_moe = (DEPTH + 1) // 2, DEPTH // 2
    HD = N_HEADS * HEAD_DIM
    return {
        'x': jax.random.normal(ks.pop(), (BATCH, SEQ, D_MODEL), jnp.float32),
        'mixer_norm': gain((DEPTH, D_MODEL)),
        'ffn_norm': gain((DEPTH, D_MODEL)),
        'final_norm': gain((D_MODEL,)),
        'a_wqkv': w((D_MODEL, QKV_DIM), D_MODEL),
        'a_q_gain': gain((HEAD_DIM,)),
        'a_k_gain': gain((HEAD_DIM,)),
        'a_wo': w((HD, D_MODEL), HD),
        'b_wqkv': w((D_MODEL, QKV_DIM), D_MODEL),
        'b_sink': 0.5 * jax.random.normal(ks.pop(), (N_HEADS,), jnp.float32),
        'b_wo': w((HD, D_MODEL), HD),
        'c_wdq': w((D_MODEL, MLA_Q_RANK), D_MODEL),
        'c_q_gain': gain((MLA_Q_RANK,)),
        'c_wuq': w((MLA_Q_RANK, MLA_HEADS * (MLA_NOPE_DIM + MLA_ROPE_DIM)), MLA_Q_RANK),
        'c_wdkv': w((D_MODEL, MLA_KV_RANK + MLA_ROPE_DIM), D_MODEL),
        'c_kv_gain': gain((MLA_KV_RANK,)),
        'c_wukv': w((MLA_KV_RANK, MLA_HEADS * (MLA_NOPE_DIM + MLA_V_DIM)), MLA_KV_RANK),
        'c_wo': w((MLA_HEADS * MLA_V_DIM, D_MODEL), MLA_HEADS * MLA_V_DIM),
        'd_wqkv': w((D_MODEL, QKV_DIM), D_MODEL),
        'd_wo': w((HD, D_MODEL), HD),
        'ffn_wg': w((n_dense, D_MODEL, FFN_DIM), D_MODEL),
        'ffn_wu': w((n_dense, D_MODEL, FFN_DIM), D_MODEL),
        'ffn_wd': w((n_dense, FFN_DIM, D_MODEL), FFN_DIM),
        'moe_router': w((n_moe, D_MODEL, N_EXPERTS), D_MODEL),
        'moe_wg': w((n_moe, N_EXPERTS, D_MODEL, EXPERT_DIM), D_MODEL),
        'moe_wu': w((n_moe, N_EXPERTS, D_MODEL, EXPERT_DIM), D_MODEL),
        'moe_wd': w((n_moe, N_EXPERTS, EXPERT_DIM, D_MODEL), EXPERT_DIM),
    }


def reference(x, mixer_norm, ffn_norm, final_norm,
              a_wqkv, a_q_gain, a_k_gain, a_wo,
              b_wqkv, b_sink, b_wo,
              c_wdq, c_q_gain, c_wuq, c_wdkv, c_kv_gain, c_wukv, c_wo,
              d_wqkv, d_wo,
              ffn_wg, ffn_wu, ffn_wd,
              moe_router, moe_wg, moe_wu, moe_wd):
    S = x.shape[1]
    pos = jnp.arange(S, dtype=jnp.int32)
    ang_partial = rope_angles(pos, PARTIAL_ROPE_DIM, ROPE_THETA)
    ang_mla = rope_angles(pos, MLA_ROPE_DIM, ROPE_THETA)
    for i in range(DEPTH):
        hn = rms_norm(x, mixer_norm[i])
        kind = i % N_MIXERS
        if kind == 0:
            mix = mixer_axial_gqa(hn, a_wqkv, a_q_gain, a_k_gain, a_wo)
        elif kind == 1:
            mix = mixer_window_sink(hn, b_wqkv, b_sink, b_wo, ang_partial)
        elif kind == 2:
            mix = mixer_mla(hn, c_wdq, c_q_gain, c_wuq, c_wdkv, c_kv_gain, c_wukv, c_wo, ang_mla)
        else:
            mix = mixer_dilated(hn, d_wqkv, d_wo, ang_partial)
        x = x + mix
        hn = rms_norm(x, ffn_norm[i])
        j = i // 2
        if i % 2 == 0:
            x = x + swiglu(hn, ffn_wg[j], ffn_wu[j], ffn_wd[j])
        else:
            x = x + moe_swiglu(hn, moe_router[j], moe_wg[j], moe_wu[j], moe_wd[j])
    return rms_norm(x, final_norm)
```

```python
import functools

import numpy as np
import jax
import jax.numpy as jnp
from jax import lax
from jax.experimental import pallas as pl
from jax.experimental.pallas import tpu as pltpu

F32 = jnp.float32
BF16 = jnp.bfloat16

LANE = 128
V7X_VMEM_BYTES = 64 * 1024 * 1024
VMEM_LIMIT = V7X_VMEM_BYTES - 8 * 1024 * 1024

HEAD_DIM = 128
N_HEADS = 32
N_KV_HEADS = 8
GQA_GROUP = N_HEADS // N_KV_HEADS
RMS_EPS = 1e-6
ROPE_THETA = 500000.0
AXIAL_THETA = 10000.0
GRID_W = 64
PARTIAL_ROPE_DIM = HEAD_DIM // 4
WINDOW_HALF = 128
MLA_HEADS = 32
MLA_KV_RANK = 512
MLA_NOPE_DIM = 128
MLA_ROPE_DIM = 64
MLA_QK_PAD = 256
MLA_DKV_PAD = 640
DIL_PATTERNS = ((128, 1), (512, 4), (2048, 16))
DIL_KV_HEADS = (2, 3, 3)
N_EXPERTS = 8
TOP_K = 2
Q_BLOCK = 128


def _params(*sem):
    return pltpu.CompilerParams(dimension_semantics=sem, vmem_limit_bytes=VMEM_LIMIT)


def _rmsnorm_kernel(x_ref, g_ref, o_ref):
    x = x_ref[...].astype(F32)
    ms = jnp.mean(x * x, axis=-1, keepdims=True)
    o_ref[...] = (x * lax.rsqrt(ms + RMS_EPS) * g_ref[...]).astype(o_ref.dtype)


def rmsnorm(x, g, out_dtype, tr=256):
    S, D = x.shape
    return pl.pallas_call(
        _rmsnorm_kernel,
        grid=(S // tr,),
        in_specs=[pl.BlockSpec((tr, D), lambda i: (i, 0)),
                  pl.BlockSpec((1, D), lambda i: (0, 0))],
        out_specs=pl.BlockSpec((tr, D), lambda i: (i, 0)),
        out_shape=jax.ShapeDtypeStruct((S, D), out_dtype),
        compiler_params=_params("parallel"),
    )(x, g.reshape(1, D).astype(F32))


def _rmsnorm_router_kernel(x_ref, g_ref, r_ref, o_ref, gate_ref):
    x = x_ref[...]
    ms = jnp.mean(x * x, axis=-1, keepdims=True)
    hn = x * lax.rsqrt(ms + RMS_EPS) * g_ref[...]
    o_ref[...] = hn.astype(o_ref.dtype)
    logits = jnp.dot(hn, r_ref[...], preferred_element_type=F32,
                     precision=lax.Precision.HIGHEST)
    lane = lax.broadcasted_iota(jnp.int32, logits.shape, 1)
    logits = jnp.where(lane < N_EXPERTS, logits, -jnp.inf)
    v1 = jnp.max(logits, axis=-1, keepdims=True)
    i1 = jnp.min(jnp.where(logits == v1, lane, LANE), axis=-1, keepdims=True)
    rest = jnp.where(lane == i1, -jnp.inf, logits)
    v2 = jnp.max(rest, axis=-1, keepdims=True)
    i2 = jnp.min(jnp.where(rest == v2, lane, LANE), axis=-1, keepdims=True)
    e2 = jnp.exp(v2 - v1)
    g1 = 1.0 / (1.0 + e2)
    g2 = e2 / (1.0 + e2)
    gate_ref[...] = jnp.where(lane == i1, g1, jnp.where(lane == i2, g2, 0.0))


def rmsnorm_router(x, g, router, tr=256):
    S, D = x.shape
    rp = jnp.zeros((D, LANE), F32).at[:, :N_EXPERTS].set(router.astype(F32))
    return pl.pallas_call(
        _rmsnorm_router_kernel,
        grid=(S // tr,),
        in_specs=[pl.BlockSpec((tr, D), lambda i: (i, 0)),
                  pl.BlockSpec((1, D), lambda i: (0, 0)),
                  pl.BlockSpec((D, LANE), lambda i: (0, 0))],
        out_specs=[pl.BlockSpec((tr, D), lambda i: (i, 0)),
                   pl.BlockSpec((tr, LANE), lambda i: (i, 0))],
        out_shape=[jax.ShapeDtypeStruct((S, D), BF16),
                   jax.ShapeDtypeStruct((S, LANE), F32)],
        compiler_params=_params("parallel"),
    )(x, g.reshape(1, D).astype(F32), rp)


def _mm_kernel(a_ref, w_ref, *rest, nk, has_res):
    if has_res:
        res_ref, o_ref, *scratch = rest
    else:
        o_ref, *scratch = rest
    p = jnp.dot(a_ref[...], w_ref[...], preferred_element_type=F32)
    if nk == 1:
        if has_res:
            p = p + res_ref[...]
        o_ref[...] = p.astype(o_ref.dtype)
        return
    acc_ref, = scratch
    k = pl.program_id(2)

    @pl.when(k == 0)
    def _():
        acc_ref[...] = p

    @pl.when(k > 0)
    def _():
        acc_ref[...] += p

    @pl.when(k == nk - 1)
    def _():
        r = acc_ref[...]
        if has_res:
            r = r + res_ref[...]
        o_ref[...] = r.astype(o_ref.dtype)


def matmul(a, w, res=None, out_dtype=BF16, bm=1024, bn=1024, bk=None):
    M, K = a.shape
    _, N = w.shape
    bm, bn = min(bm, M), min(bn, N)
    bk = K if bk is None else bk
    nk = K // bk
    in_specs = [pl.BlockSpec((bm, bk), lambda i, j, k: (i, k)),
                pl.BlockSpec((bk, bn), lambda i, j, k: (k, j))]
    args = [a, w]
    if res is not None:
        in_specs.append(pl.BlockSpec((bm, bn), lambda i, j, k: (i, j)))
        args.append(res)
    return pl.pallas_call(
        functools.partial(_mm_kernel, nk=nk, has_res=res is not None),
        grid=(M // bm, N // bn, nk),
        in_specs=in_specs,
        out_specs=pl.BlockSpec((bm, bn), lambda i, j, k: (i, j)),
        out_shape=jax.ShapeDtypeStruct((M, N), out_dtype),
        scratch_shapes=[pltpu.VMEM((bm, bn), F32)] if nk > 1 else [],
        compiler_params=_params("parallel", "parallel", "arbitrary"),
    )(*args)


def _glu_kernel(a_ref, wg_ref, wu_ref, *rest, tiles_per_expert):
    if tiles_per_expert:
        gate_ref, o_ref = rest
    else:
        o_ref, = rest
    a = a_ref[...]
    g = jnp.dot(a, wg_ref[...], preferred_element_type=F32)
    u = jnp.dot(a, wu_ref[...], preferred_element_type=F32)
    act = g * jax.nn.sigmoid(g) * u
    if tiles_per_expert:
        e = pl.program_id(1) // tiles_per_expert
        gates = gate_ref[...]
        lane = lax.broadcasted_iota(jnp.int32, gates.shape, 1)
        act = act * jnp.sum(jnp.where(lane == e, gates, 0.0), axis=-1, keepdims=True)
    o_ref[...] = act.astype(o_ref.dtype)


def glu_dense(a, wg, wu, bm=1024, bn=512):
    M, K = a.shape
    N = wg.shape[1]
    return pl.pallas_call(
        functools.partial(_glu_kernel, tiles_per_expert=0),
        grid=(M // bm, N // bn),
        in_specs=[pl.BlockSpec((bm, K), lambda i, j: (i, 0)),
                  pl.BlockSpec((K, bn), lambda i, j: (0, j)),
                  pl.BlockSpec((K, bn), lambda i, j: (0, j))],
        out_specs=pl.BlockSpec((bm, bn), lambda i, j: (i, j)),
        out_shape=jax.ShapeDtypeStruct((M, N), BF16),
        compiler_params=_params("parallel", "parallel"),
    )(a, wg, wu)


def glu_experts(a, wg, wu, gates, bm=512):
    M, K = a.shape
    E, _, F = wg.shape
    tpe = 2
    bn = F // tpe
    return pl.pallas_call(
        functools.partial(_glu_kernel, tiles_per_expert=tpe),
        grid=(M // bm, E * tpe),
        in_specs=[pl.BlockSpec((bm, K), lambda i, j: (i, 0)),
                  pl.BlockSpec((None, K, bn), lambda i, j: (j // tpe, 0, j % tpe)),
                  pl.BlockSpec((None, K, bn), lambda i, j: (j // tpe, 0, j % tpe)),
                  pl.BlockSpec((bm, LANE), lambda i, j: (i, 0))],
        out_specs=pl.BlockSpec((bm, bn), lambda i, j: (i, j)),
        out_shape=jax.ShapeDtypeStruct((M, E * F), BF16),
        compiler_params=_params("parallel", "parallel"),
    )(a, wg, wu, gates)


def _rotate(x, cos, sin, half):
    lane = lax.broadcasted_iota(jnp.int32, x.shape, 1)
    partner = jnp.where((lane % (2 * half)) < half,
                        pltpu.roll(x, LANE - half, 1), pltpu.roll(x, half, 1))
    return x * cos + partner * sin


def _prep_kernel(x_ref, mult_ref, cos_ref, sin_ref, o_ref, *, n_tiles, use_norm, half, period):
    for t in range(n_tiles):
        sl = slice(t * LANE, (t + 1) * LANE)
        x = x_ref[:, sl].astype(F32)
        if use_norm:
            x = x * lax.rsqrt(jnp.mean(x * x, axis=-1, keepdims=True) + RMS_EPS)
        x = x * mult_ref[:, sl]
        tp = t % period
        tsl = slice(tp * LANE, (tp + 1) * LANE)
        o_ref[:, sl] = _rotate(x, cos_ref[:, tsl], sin_ref[:, tsl], half).astype(o_ref.dtype)


def rotary_prep(x, ncols, mult, cos, sin, *, use_norm, half, tr=256, bc=1024):
    S = x.shape[0]
    period = cos.shape[1] // LANE
    return pl.pallas_call(
        functools.partial(_prep_kernel, n_tiles=bc // LANE, use_norm=use_norm, half=half, period=period),
        grid=(S // tr, ncols // bc),
        in_specs=[pl.BlockSpec((tr, bc), lambda i, j: (i, j)),
                  pl.BlockSpec((1, bc), lambda i, j: (0, j)),
                  pl.BlockSpec((tr, period * LANE), lambda i, j: (i, 0)),
                  pl.BlockSpec((tr, period * LANE), lambda i, j: (i, 0))],
        out_specs=pl.BlockSpec((tr, bc), lambda i, j: (i, j)),
        out_shape=jax.ShapeDtypeStruct((S, ncols), BF16),
        compiler_params=_params("parallel", "parallel"),
    )(x, mult.reshape(1, ncols).astype(F32), cos, sin)


def _rope_tables(ang, width):
    S, r2 = ang.shape
    c, s = jnp.cos(ang), jnp.sin(ang)
    pad = width - 2 * r2
    cos = jnp.concatenate([c, c, jnp.ones((S, pad), F32)], axis=1)
    sin = jnp.concatenate([-s, s, jnp.zeros((S, pad), F32)], axis=1)
    return cos, sin


def _angles(pos, dim, theta):
    inv = 1.0 / (theta ** (jnp.arange(0, dim, 2, dtype=F32) / dim))
    return pos.astype(F32)[:, None] * inv[None, :]


def _mla_dkv_kernel(x_ref, g_ref, cos_ref, sin_ref, o_ref):
    ckv = x_ref[:, :MLA_KV_RANK]
    ms = jnp.mean(ckv * ckv, axis=-1, keepdims=True)
    o_ref[:, :MLA_KV_RANK] = (ckv * lax.rsqrt(ms + RMS_EPS) * g_ref[...]).astype(o_ref.dtype)
    kr = x_ref[:, MLA_KV_RANK:]
    o_ref[:, MLA_KV_RANK:] = _rotate(kr, cos_ref[...], sin_ref[...], MLA_ROPE_DIM // 2).astype(o_ref.dtype)


def mla_dkv_prep(dkv, gain, cos, sin, tr=256):
    S = dkv.shape[0]
    return pl.pallas_call(
        _mla_dkv_kernel,
        grid=(S // tr,),
        in_specs=[pl.BlockSpec((tr, MLA_DKV_PAD), lambda i: (i, 0)),
                  pl.BlockSpec((1, MLA_KV_RANK), lambda i: (0, 0)),
                  pl.BlockSpec((tr, LANE), lambda i: (i, 0)),
                  pl.BlockSpec((tr, LANE), lambda i: (i, 0))],
        out_specs=pl.BlockSpec((tr, MLA_DKV_PAD), lambda i: (i, 0)),
        out_shape=jax.ShapeDtypeStruct((S, MLA_DKV_PAD), BF16),
        compiler_params=_params("parallel"),
    )(dkv, gain.reshape(1, MLA_KV_RANK).astype(F32), cos, sin)


def _flash_kernel(q_ref, k_ref, v_ref, o_ref, *, tk):
    q = q_ref[...]
    tq = q.shape[0]
    dv = v_ref.shape[1]

    def body(c, carry):
        m, l, acc = carry
        start = pl.multiple_of(c * tk, tk)
        k = k_ref[pl.ds(start, tk), :]
        v = v_ref[pl.ds(start, tk), :]
        s = lax.dot_general(q, k, (((1,), (1,)), ((), ())), preferred_element_type=F32)
        m_new = jnp.maximum(m, jnp.max(s, axis=-1, keepdims=True))
        p = jnp.exp(s - m_new)
        alpha = jnp.exp(m - m_new)
        l = alpha * l + jnp.sum(p, axis=-1, keepdims=True)
        acc = alpha * acc + jnp.dot(p.astype(v.dtype), v, preferred_element_type=F32)
        return m_new, l, acc

    init = (jnp.full((tq, 1), -jnp.inf, F32), jnp.zeros((tq, 1), F32), jnp.zeros((tq, dv), F32))
    _, l, acc = lax.fori_loop(0, k_ref.shape[0] // tk, body, init)
    o_ref[...] = (acc / l).astype(o_ref.dtype)


def flash_attention(q, k, v, *, n_heads, dk, k_block, v_block, tq=512, tk=512):
    S = q.shape[0]
    return pl.pallas_call(
        functools.partial(_flash_kernel, tk=tk),
        grid=(n_heads, S // tq),
        in_specs=[pl.BlockSpec((tq, dk), lambda h, i: (i, h)),
                  pl.BlockSpec((S, dk), lambda h, i: (0, k_block(h))),
                  pl.BlockSpec((S, HEAD_DIM), lambda h, i: (0, v_block(h)))],
        out_specs=pl.BlockSpec((tq, HEAD_DIM), lambda h, i: (i, h)),
        out_shape=jax.ShapeDtypeStruct((S, n_heads * HEAD_DIM), BF16),
        compiler_params=_params("parallel", "parallel"),
    )(q, k, v)


def _band_kernel(q_ref, kp_ref, kc_ref, kn_ref, vp_ref, vc_ref, vn_ref, sink_ref, o_ref, *rest,
                 window, n_blocks):
    i = pl.program_id(2)
    B = Q_BLOCK
    qs = jnp.concatenate([q_ref[:, h * B:(h + 1) * B] for h in range(GQA_GROUP)], axis=0)
    k = jnp.concatenate([kp_ref[...], kc_ref[...], kn_ref[...]], axis=0)
    v = jnp.concatenate([vp_ref[...], vc_ref[...], vn_ref[...]], axis=0)
    s = lax.dot_general(qs, k, (((1,), (1,)), ((), ())), preferred_element_type=F32)
    row = lax.broadcasted_iota(jnp.int32, s.shape, 0) % B
    col = lax.broadcasted_iota(jnp.int32, s.shape, 1)
    rel = col - B - row
    kpos = (i - 1) * B + col
    valid = (jnp.abs(rel) <= window) & (kpos >= 0) & (kpos < n_blocks * B)
    s = jnp.where(valid, s, -jnp.inf)
    sink = jnp.concatenate(
        [jnp.broadcast_to(sink_ref[:, h * B:h * B + 1], (B, 1)) for h in range(GQA_GROUP)], axis=0)
    m = jnp.maximum(jnp.max(s, axis=-1, keepdims=True), sink)
    p = jnp.exp(s - m)
    den = jnp.sum(p, axis=-1, keepdims=True) + jnp.exp(sink - m)
    o = jnp.dot(p.astype(v.dtype), v, preferred_element_type=F32) / den
    for h in range(GQA_GROUP):
        o_ref[:, h * B:(h + 1) * B] = o[h * B:(h + 1) * B, :].astype(o_ref.dtype)
    if rest:
        lse_ref, = rest
        lse = m + jnp.log(den)
        lane = lax.broadcasted_iota(jnp.int32, (B, LANE), 1)
        tile = jnp.zeros((B, LANE), F32)
        for h in range(GQA_GROUP):
            tile = jnp.where(lane == h, lse[h * B:(h + 1) * B, :], tile)
        lse_ref[...] = tile


def band_attention(qk, qkv, sink, *, kv0, n_kv, dilation, window, with_lse):
    S = qk.shape[0]
    d = dilation
    B = Q_BLOCK
    nb = S // d // B
    cq = qk.shape[1] // (GQA_GROUP * HEAD_DIM)
    ck = qk.shape[1] // HEAD_DIM
    cv = qkv.shape[1] // HEAD_DIM
    qk2 = qk.reshape(S // d, d * qk.shape[1])
    qkv2 = qkv.reshape(S // d, d * qkv.shape[1])
    k_col = lambda r, g: r * ck + N_HEADS + kv0 + g
    v_col = lambda r, g: r * cv + N_HEADS + N_KV_HEADS + kv0 + g
    prev = lambda i: jnp.maximum(i - 1, 0)
    nxt = lambda i: jnp.minimum(i + 1, nb - 1)
    in_specs = [
        pl.BlockSpec((B, GQA_GROUP * HEAD_DIM), lambda r, g, i: (i, r * cq + kv0 + g)),
        pl.BlockSpec((B, HEAD_DIM), lambda r, g, i: (prev(i), k_col(r, g))),
        pl.BlockSpec((B, HEAD_DIM), lambda r, g, i: (i, k_col(r, g))),
        pl.BlockSpec((B, HEAD_DIM), lambda r, g, i: (nxt(i), k_col(r, g))),
        pl.BlockSpec((B, HEAD_DIM), lambda r, g, i: (prev(i), v_col(r, g))),
        pl.BlockSpec((B, HEAD_DIM), lambda r, g, i: (i, v_col(r, g))),
        pl.BlockSpec((B, HEAD_DIM), lambda r, g, i: (nxt(i), v_col(r, g))),
        pl.BlockSpec((None, 1, GQA_GROUP * HEAD_DIM), lambda r, g, i: (kv0 + g, 0, 0)),
    ]
    out_specs = [pl.BlockSpec((B, GQA_GROUP * HEAD_DIM), lambda r, g, i: (i, r * n_kv + g))]
    out_shape = [jax.ShapeDtypeStruct((S // d, d * n_kv * GQA_GROUP * HEAD_DIM), BF16)]
    if with_lse:
        out_specs.append(pl.BlockSpec((B, LANE), lambda r, g, i: (i, r * n_kv + g)))
        out_shape.append(jax.ShapeDtypeStruct((S // d, d * n_kv * LANE), F32))
    outs = pl.pallas_call(
        functools.partial(_band_kernel, window=window, n_blocks=nb),
        grid=(d, n_kv, nb),
        in_specs=in_specs,
        out_specs=out_specs,
        out_shape=out_shape,
        compiler_params=_params("parallel", "parallel", "parallel"),
    )(qk2, qk2, qk2, qk2, qkv2, qkv2, qkv2, sink)
    o = outs[0].reshape(S, n_kv * GQA_GROUP * HEAD_DIM)
    if with_lse:
        return o, outs[1].reshape(S, n_kv * LANE)
    return o


def _sink_rows(sink):
    s = jnp.repeat(sink.astype(F32), HEAD_DIM)
    return s.reshape(N_KV_HEADS, 1, GQA_GROUP * HEAD_DIM)


def _dil_merge_kernel(*refs):
    n = len(DIL_KV_HEADS)
    o_refs, lse_refs, out_ref = refs[:n], refs[n:2 * n], refs[2 * n]
    log_den = []
    for g, nkv in enumerate(DIL_KV_HEADS):
        lse = lse_refs[g][...]
        lane = lax.broadcasted_iota(jnp.int32, lse.shape, 1) % LANE
        lse = jnp.where(lane < GQA_GROUP, lse, -jnp.inf)
        m = jnp.max(lse, axis=-1, keepdims=True)
        tot = jnp.sum(jnp.exp(lse - m), axis=-1, keepdims=True)
        log_den.append(m + jnp.log(tot) - np.float32(np.log(nkv * GQA_GROUP)))
    mx = functools.reduce(jnp.maximum, log_den)
    ex = [jnp.exp(ld - mx) for ld in log_den]
    tot = functools.reduce(lambda a, b: a + b, ex)
    col = 0
    for g, nkv in enumerate(DIL_KV_HEADS):
        w = ex[g] / tot * np.float32(n)
        width = nkv * GQA_GROUP * HEAD_DIM
        out_ref[:, col:col + width] = (o_refs[g][...].astype(F32) * w).astype(out_ref.dtype)
        col += width


def dil_merge(os_, lses, tr=256):
    S = os_[0].shape[0]
    widths = [o.shape[1] for o in os_]
    in_specs = [pl.BlockSpec((tr, w), lambda i: (i, 0)) for w in widths]
    in_specs += [pl.BlockSpec((tr, l.shape[1]), lambda i: (i, 0)) for l in lses]
    total = sum(widths)
    return pl.pallas_call(
        _dil_merge_kernel,
        grid=(S // tr,),
        in_specs=in_specs,
        out_specs=pl.BlockSpec((tr, total), lambda i: (i, 0)),
        out_shape=jax.ShapeDtypeStruct((S, total), BF16),
        compiler_params=_params("parallel"),
    )(*os_, *lses)


def _qk_mult(q_mult, k_mult):
    return jnp.concatenate([jnp.tile(q_mult, N_HEADS), jnp.tile(k_mult, N_KV_HEADS)])


def _mixer_axial(hn, wqkv, q_gain, k_gain, wo, x):
    S = hn.shape[0]
    qkv = matmul(hn, wqkv)
    t = jnp.arange(S, dtype=jnp.int32)
    half = HEAD_DIM // 2
    cr, sr = _rope_tables(_angles(t // GRID_W, half, AXIAL_THETA), half)
    cc, sc = _rope_tables(_angles(t % GRID_W, half, AXIAL_THETA), half)
    cos, sin = jnp.concatenate([cr, cc], axis=1), jnp.concatenate([sr, sc], axis=1)
    scale = HEAD_DIM ** -0.5
    mult = _qk_mult(q_gain.astype(F32) * scale, k_gain.astype(F32))
    nqk = (N_HEADS + N_KV_HEADS) * HEAD_DIM
    qk = rotary_prep(qkv, nqk, mult, cos, sin, use_norm=True, half=half // 2)
    o = flash_attention(qk, qk, qkv, n_heads=N_HEADS, dk=HEAD_DIM,
                        k_block=lambda h: N_HEADS + h // GQA_GROUP,
                        v_block=lambda h: N_HEADS + N_KV_HEADS + h // GQA_GROUP)
    return matmul(o, wo, res=x, out_dtype=F32, bn=512)


def _partial_rope_qk(qkv, cos, sin):
    scale = HEAD_DIM ** -0.5
    mult = _qk_mult(jnp.full((HEAD_DIM,), scale, F32), jnp.ones((HEAD_DIM,), F32))
    nqk = (N_HEADS + N_KV_HEADS) * HEAD_DIM
    return rotary_prep(qkv, nqk, mult, cos, sin, use_norm=False, half=PARTIAL_ROPE_DIM // 2)


def _mixer_window(hn, wqkv, sink, wo, x, cos, sin):
    qkv = matmul(hn, wqkv)
    qk = _partial_rope_qk(qkv, cos, sin)
    o = band_attention(qk, qkv, _sink_rows(sink), kv0=0, n_kv=N_KV_HEADS, dilation=1,
                       window=WINDOW_HALF, with_lse=False)
    return matmul(o, wo, res=x, out_dtype=F32, bn=512)


def _mixer_mla(hn, wdq, q_gain, wuq, wdkv, kv_gain, wukv, wo, x, cos, sin):
    S = hn.shape[0]
    H, dn, dr = MLA_HEADS, MLA_NOPE_DIM, MLA_ROPE_DIM
    cq = rmsnorm(matmul(hn, wdq, out_dtype=F32), q_gain, BF16)
    wuq_p = jnp.pad(wuq.reshape(-1, H, dn + dr), ((0, 0), (0, 0), (0, MLA_QK_PAD - dn - dr)))
    q = matmul(cq, wuq_p.reshape(-1, H * MLA_QK_PAD))
    ones, zeros = jnp.ones((S, LANE), F32), jnp.zeros((S, LANE), F32)
    scale = (dn + dr) ** -0.5
    q = rotary_prep(q, H * MLA_QK_PAD, jnp.full((H * MLA_QK_PAD,), scale, F32),
                    jnp.concatenate([ones, cos], axis=1), jnp.concatenate([zeros, sin], axis=1),
                    use_norm=False, half=dr // 2)
    wdkv_p = jnp.pad(wdkv, ((0, 0), (0, MLA_DKV_PAD - wdkv.shape[1])))
    ckr = mla_dkv_prep(matmul(hn, wdkv_p, out_dtype=F32, bn=MLA_DKV_PAD), kv_gain, cos, sin)
    wk = wukv.reshape(-1, H, 2 * dn)[:, :, :dn]
    wk = jnp.pad(wk, ((0, MLA_DKV_PAD - MLA_KV_RANK), (0, 0), (0, MLA_QK_PAD - dn)))
    eye = jnp.eye(dr, dtype=wukv.dtype)[:, None, :]
    wk = wk.at[MLA_KV_RANK:MLA_KV_RANK + dr, :, dn:dn + dr].set(jnp.broadcast_to(eye, (dr, H, dr)))
    kfull = matmul(ckr, wk.reshape(MLA_DKV_PAD, H * MLA_QK_PAD))
    wv = wukv.reshape(-1, H, 2 * dn)[:, :, dn:].reshape(-1, H * dn)
    v = matmul(ckr, jnp.pad(wv, ((0, MLA_DKV_PAD - MLA_KV_RANK), (0, 0))))
    o = flash_attention(q, kfull, v, n_heads=H, dk=MLA_QK_PAD,
                        k_block=lambda h: h, v_block=lambda h: h)
    return matmul(o, wo, res=x, out_dtype=F32, bn=512)


def _mixer_dilated(hn, wqkv, wo, x, cos, sin):
    qkv = matmul(hn, wqkv)
    qk = _partial_rope_qk(qkv, cos, sin)
    no_sink = jnp.full((N_KV_HEADS, 1, GQA_GROUP * HEAD_DIM), -jnp.inf, F32)
    os_, lses, kv0 = [], [], 0
    for (w, d), nkv in zip(DIL_PATTERNS, DIL_KV_HEADS):
        o, lse = band_attention(qk, qkv, no_sink, kv0=kv0, n_kv=nkv, dilation=d,
                                window=(w // 2) // d, with_lse=True)
        os_.append(o)
        lses.append(lse)
        kv0 += nkv
    return matmul(dil_merge(os_, lses), wo, res=x, out_dtype=F32, bn=512)


def _ffn_dense(x, g, wg, wu, wd):
    hn = rmsnorm(x, g, BF16)
    act = glu_dense(hn, wg, wu)
    return matmul(act, wd, res=x, out_dtype=F32, bk=2048)


def _ffn_moe(x, g, router, wg, wu, wd):
    hn, gates = rmsnorm_router(x, g, router)
    act = glu_experts(hn, wg, wu, gates)
    E, F, D = wd.shape
    return matmul(act, wd.reshape(E * F, D), res=x, out_dtype=F32, bk=2048)


def kernel(x, mixer_norm, ffn_norm, final_norm, a_wqkv, a_q_gain, a_k_gain, a_wo, b_wqkv, b_sink, b_wo,
           c_wdq, c_q_gain, c_wuq, c_wdkv, c_kv_gain, c_wukv, c_wo, d_wqkv, d_wo, ffn_wg, ffn_wu, ffn_wd,
           moe_router, moe_wg, moe_wu, moe_wd):
    B, S, D = x.shape
    pos = jnp.arange(S, dtype=jnp.int32)
    cos_p, sin_p = _rope_tables(_angles(pos, PARTIAL_ROPE_DIM, ROPE_THETA), LANE)
    cos_m, sin_m = _rope_tables(_angles(pos, MLA_ROPE_DIM, ROPE_THETA), LANE)
    bf = lambda w: w.astype(BF16)
    outs = []
    for b in range(B):
        h = x[b].astype(F32)
        h = _mixer_axial(rmsnorm(h, mixer_norm[0], BF16), bf(a_wqkv), a_q_gain, a_k_gain, bf(a_wo), h)
        h = _ffn_dense(h, ffn_norm[0], bf(ffn_wg[0]), bf(ffn_wu[0]), bf(ffn_wd[0]))
        h = _mixer_window(rmsnorm(h, mixer_norm[1], BF16), bf(b_wqkv), b_sink, bf(b_wo), h, cos_p, sin_p)
        h = _ffn_moe(h, ffn_norm[1], moe_router[0], bf(moe_wg[0]), bf(moe_wu[0]), bf(moe_wd[0]))
        h = _mixer_mla(rmsnorm(h, mixer_norm[2], BF16), bf(c_wdq), c_q_gain, bf(c_wuq), bf(c_wdkv),
                       c_kv_gain, bf(c_wukv), bf(c_wo), h, cos_m, sin_m)
        h = _ffn_dense(h, ffn_norm[2], bf(ffn_wg[1]), bf(ffn_wu[1]), bf(ffn_wd[1]))
        h = _mixer_dilated(rmsnorm(h, mixer_norm[3], BF16), bf(d_wqkv), bf(d_wo), h, cos_p, sin_p)
        h = _ffn_moe(h, ffn_norm[3], moe_router[1], bf(moe_wg[1]), bf(moe_wu[1]), bf(moe_wd[1]))
        outs.append(rmsnorm(h, final_norm, x.dtype))
    return outs[0][None] if B == 1 else jnp.stack(outs)
```
